```python
import math
import jax, jax.numpy as jnp
from jax import lax
import numpy as np


D_MODEL = 2048
BATCH = 2
SEQ = 16384
DEPTH = 2
DEC_BATCH = 16
DEC_SEQ = 32
PAST_LEN = 2048

CHUNK = 64
N_A_LAYERS = DEPTH // 2
N_B_LAYERS = DEPTH - N_A_LAYERS
SSM_GROUP = 16
N_GROUPS = D_MODEL // SSM_GROUP
SSM_STATE = 64
SCAN_BLOCK = CHUNK
N_HEADS = 8
HEAD_DIM = D_MODEL // (2 * N_HEADS)
V_DIM = 2 * HEAD_DIM
ROT_DIM = HEAD_DIM // 4
ROPE_THETA = 500000.0
Q_BLOCK = 128
D_FF = 5632
CONV_W = 3
PLE_DIM = 256
EPS = 1e-6
SUBLN_EPS = 1e-5

kernel_name = 'yoco_s5_diffattn_stream_step'


def _rmsnorm(x, g, eps=EPS):
    xf = x.astype(jnp.float32)
    y = xf * lax.rsqrt(jnp.mean(xf * xf, axis=-1, keepdims=True) + eps)
    return (y * g.astype(jnp.float32)).astype(x.dtype)


def _rope_partial(x, pos):
    half = ROT_DIM // 2
    inv = ROPE_THETA ** (-jnp.arange(half, dtype=jnp.float32) * 2.0 / ROT_DIM)
    ang = pos.astype(jnp.float32)[:, None] * inv[None, :]
    cos = jnp.cos(ang)[None, :, None, None, :]
    sin = jnp.sin(ang)[None, :, None, None, :]
    xf = x.astype(jnp.float32)
    x1 = xf[..., :half]
    x2 = xf[..., half:ROT_DIM]
    out = jnp.concatenate([x1 * cos - x2 * sin, x2 * cos + x1 * sin, xf[..., ROT_DIM:]], axis=-1)
    return out.astype(x.dtype)


def _s5_scan(u, s_re, s_im, a_re, a_im, log_dt, b_re, b_im, c_re, c_im, d_skip):
    f32 = jnp.float32
    bt, L, _ = u.shape
    uf = u.astype(f32).reshape(bt, L, N_GROUPS, SSM_GROUP)
    a_re = a_re.astype(f32)
    a_im = a_im.astype(f32)
    dt = jnp.exp(log_dt.astype(f32))[:, None]
    mag = jnp.exp(a_re * dt)
    lb_re = mag * jnp.cos(a_im * dt)
    lb_im = mag * jnp.sin(a_im * dt)
    den = a_re * a_re + a_im * a_im
    n_re = lb_re - 1.0
    k_re = (n_re * a_re + lb_im * a_im) / den
    k_im = (lb_im * a_re - n_re * a_im) / den
    b_re = b_re.astype(f32)
    b_im = b_im.astype(f32)
    bb_re = k_re[..., None] * b_re - k_im[..., None] * b_im
    bb_im = k_re[..., None] * b_im + k_im[..., None] * b_re
    c_re = c_re.astype(f32)
    c_im = c_im.astype(f32)
    blk = SCAN_BLOCK if L % SCAN_BLOCK == 0 else L
    nb = L // blk
    ub = uf.reshape(bt, nb, blk, N_GROUPS, SSM_GROUP).swapaxes(0, 1)
    a_blk_re = jnp.broadcast_to(lb_re, (bt, blk, N_GROUPS, SSM_STATE))
    a_blk_im = jnp.broadcast_to(lb_im, (bt, blk, N_GROUPS, SSM_STATE))

    def combine(e1, e2):
        a1r, a1i, b1r, b1i = e1
        a2r, a2i, b2r, b2i = e2
        return (a2r * a1r - a2i * a1i,
                a2r * a1i + a2i * a1r,
                a2r * b1r - a2i * b1i + b2r,
                a2r * b1i + a2i * b1r + b2i)

    def step(carry, ut):
        xr0, xi0 = carry
        bu_re = jnp.einsum('blgc,gpc->blgp', ut, bb_re)
        bu_im = jnp.einsum('blgc,gpc->blgp', ut, bb_im)
        pr, pi, sr, si = lax.associative_scan(combine, (a_blk_re, a_blk_im, bu_re, bu_im), axis=1)
        xr = sr + pr * xr0[:, None] - pi * xi0[:, None]
        xi = si + pr * xi0[:, None] + pi * xr0[:, None]
        y = jnp.einsum('blgp,gcp->blgc', xr, c_re) - jnp.einsum('blgp,gcp->blgc', xi, c_im)
        return (xr[:, -1], xi[:, -1]), y

    (fr, fi), ys = lax.scan(step, (s_re.astype(f32), s_im.astype(f32)), ub)
    y = ys.swapaxes(0, 1).reshape(bt, L, D_MODEL) + uf.reshape(bt, L, D_MODEL) * d_skip.astype(f32)
    return y.astype(u.dtype), fr, fi


def _glu(y, w_glu):
    g = jax.nn.gelu(y)
    z = g @ w_glu
    return z[..., :D_MODEL] * jax.nn.sigmoid(z[..., D_MODEL:])


def _conv_ffn(h, conv0, g, w_up, conv_w, conv_b, w_down):
    L = h.shape[1]
    z = _rmsnorm(h, g) @ w_up
    zp = jnp.concatenate([conv0.astype(z.dtype), z], axis=1)
    c = conv_b
    for k in range(CONV_W):
        c = c + conv_w[k] * zp[:, k:k + L]
    out = (jax.nn.silu(c[..., D_FF:]) * c[..., :D_FF]) @ w_down
    return out, zp[:, -(CONV_W - 1):]


def _ple(h, p_i, g, w_gate, w_proj):
    return h + (p_i @ w_proj) * jax.nn.sigmoid(_rmsnorm(h, g) @ w_gate)


def _shared_kv(h, pos, g, w_k, w_v):
    bt, L, _ = h.shape
    n = _rmsnorm(h, g)
    k = _rope_partial((n @ w_k).reshape(bt, L, N_HEADS, 2, HEAD_DIM), pos)
    v = (n @ w_v).reshape(bt, L, N_HEADS, V_DIM)
    return k, v


def _diff_attend(q, k, v, q_cid, k_cid, lam):
    f32 = jnp.float32
    scale = HEAD_DIM ** -0.5
    kf = k.astype(f32)
    vf = v.astype(f32)

    def block(args):
        qb, cb = args
        s = jnp.einsum('bqhcd,bkhcd->bhcqk', qb.astype(f32), kf) * scale
        mask = k_cid[None, :] <= cb[:, None]
        s = jnp.where(mask, s, -jnp.inf)
        pr = jax.nn.softmax(s, axis=-1)
        a = pr[:, :, 0] - lam * pr[:, :, 1]
        return jnp.einsum('bhqk,bkhv->bqhv', a, vf)

    bq, sq = q.shape[:2]
    if sq > Q_BLOCK and sq % Q_BLOCK == 0:
        nb = sq // Q_BLOCK
        qb = q.reshape(bq, nb, Q_BLOCK, N_HEADS, 2, HEAD_DIM).swapaxes(0, 1)
        cb = q_cid.reshape(nb, Q_BLOCK)
        o = lax.map(block, (qb, cb))
        return o.swapaxes(0, 1).reshape(bq, sq, N_HEADS, V_DIM)
    return block((q, q_cid))


def _diff_attn(u, pos, k_all, v_all, layer_idx, w_q, lq1, lk1, lq2, lk2, g_sub, w_o):
    bt, L, _ = u.shape
    f32 = jnp.float32
    q = _rope_partial((u @ w_q).reshape(bt, L, N_HEADS, 2, HEAD_DIM), pos)
    lam_init = 0.8 - 0.6 * math.exp(-0.3 * layer_idx)
    lam = (jnp.exp(jnp.sum(lq1.astype(f32) * lk1.astype(f32)))
           - jnp.exp(jnp.sum(lq2.astype(f32) * lk2.astype(f32))) + lam_init)
    k_cid = jnp.arange(k_all.shape[1], dtype=jnp.int32) // CHUNK
    o = _diff_attend(q, k_all, v_all, pos // CHUNK, k_cid, lam)
    o = _rmsnorm(o, g_sub, SUBLN_EPS) * (1.0 - lam_init)
    return o.reshape(bt, L, N_HEADS * V_DIM).astype(u.dtype) @ w_o


def _run_group(x, p, pos0, ssm_re0, ssm_im0, conv0, k_cache, v_cache, W):
    bt, L, _ = x.shape
    pos = pos0 + jnp.arange(L, dtype=jnp.int32)
    h = x
    ssm_re, ssm_im, conv_new = [], [], []
    k_rows = v_rows = k_all = v_all = None
    for i in range(DEPTH):
        u = _rmsnorm(h, W['norm_mix'][i])
        if i < N_A_LAYERS:
            y, sr, si = _s5_scan(u, ssm_re0[i], ssm_im0[i], W['ssm_a_re'][i], W['ssm_a_im'][i],
                                 W['ssm_log_dt'][i], W['ssm_b_re'][i], W['ssm_b_im'][i],
                                 W['ssm_c_re'][i], W['ssm_c_im'][i], W['ssm_d'][i])
            ssm_re.append(sr)
            ssm_im.append(si)
            h = h + _glu(y, W['w_glu'][i])
        else:
            if i == N_A_LAYERS:
                k_rows, v_rows = _shared_kv(h, pos, W['norm_kv'], W['w_k'], W['w_v'])
                if k_cache is None:
                    k_all, v_all = k_rows, v_rows
                else:
                    kc = k_cache.reshape(bt, k_cache.shape[1], N_HEADS, 2, HEAD_DIM).astype(k_rows.dtype)
                    k_all = jnp.concatenate([kc, k_rows], axis=1)
                    v_all = jnp.concatenate([v_cache.astype(v_rows.dtype), v_rows], axis=1)
            j = i - N_A_LAYERS
            h = h + _diff_attn(u, pos, k_all, v_all, i, W['w_q'][j], W['lambda_q1'][j],
                               W['lambda_k1'][j], W['lambda_q2'][j], W['lambda_k2'][j],
                               W['norm_sub'][j], W['w_o'][j])
        f, cs = _conv_ffn(h, conv0[i], W['norm_ffn'][i], W['w_up'][i], W['conv_w'][i],
                          W['conv_b'][i], W['w_down'][i])
        conv_new.append(cs)
        h = h + f
        h = _ple(h, p[i], W['norm_ple'][i], W['w_ple_gate'][i], W['w_ple_proj'][i])
    y = _rmsnorm(h, W['norm_final'])
    k_out = k_rows.reshape(bt, L, N_HEADS, 2 * HEAD_DIM)
    return y, jnp.stack(ssm_re), jnp.stack(ssm_im), jnp.stack(conv_new), k_out, v_rows


def setup_inputs(seed: int = 0) -> dict:
    key = jax.random.key(seed)
    ks = iter(jax.random.split(key, 48))
    f32 = jnp.float32

    def nrm(shape, scale=1.0):
        return jax.random.normal(next(ks), shape, f32) * scale

    def gain(shape):
        return 1.0 + nrm(shape, 0.02)

    qk = N_HEADS * 2 * HEAD_DIM
    vd = N_HEADS * V_DIM
    n_idx = jnp.arange(SSM_STATE, dtype=f32)
    return {
        'x_prompt': nrm((BATCH, SEQ, D_MODEL)),
        'x_sample': nrm((DEC_BATCH, DEC_SEQ, D_MODEL)),
        'state_ssm_re': nrm((N_A_LAYERS, DEC_BATCH, N_GROUPS, SSM_STATE), 0.5),
        'state_ssm_im': nrm((N_A_LAYERS, DEC_BATCH, N_GROUPS, SSM_STATE), 0.5),
        'state_conv': nrm((DEPTH, DEC_BATCH, CONV_W - 1, 2 * D_FF)),
        'cache_k': nrm((DEC_BATCH, PAST_LEN, N_HEADS, 2 * HEAD_DIM)),
        'cache_v': nrm((DEC_BATCH, PAST_LEN, N_HEADS, V_DIM)),
        'p_prompt': nrm((DEPTH, BATCH, SEQ, PLE_DIM)),
        'p_sample': nrm((DEPTH, DEC_BATCH, DEC_SEQ, PLE_DIM)),
        'norm_mix': gain((DEPTH, D_MODEL)),
        'ssm_a_re': -0.5 * jnp.exp(nrm((N_A_LAYERS, N_GROUPS, SSM_STATE), 0.02)),
        'ssm_a_im': math.pi * n_idx + nrm((N_A_LAYERS, N_GROUPS, SSM_STATE), 0.01),
        'ssm_log_dt': jax.random.uniform(next(ks), (N_A_LAYERS, N_GROUPS), f32,
                                         math.log(1e-3), math.log(1e-1)),
        'ssm_b_re': nrm((N_A_LAYERS, N_GROUPS, SSM_STATE, SSM_GROUP), (2 * SSM_GROUP) ** -0.5),
        'ssm_b_im': nrm((N_A_LAYERS, N_GROUPS, SSM_STATE, SSM_GROUP), (2 * SSM_GROUP) ** -0.5),
        'ssm_c_re': nrm((N_A_LAYERS, N_GROUPS, SSM_GROUP, SSM_STATE), SSM_STATE ** -0.5),
        'ssm_c_im': nrm((N_A_LAYERS, N_GROUPS, SSM_GROUP, SSM_STATE), SSM_STATE ** -0.5),
        'ssm_d': nrm((N_A_LAYERS, D_MODEL)),
        'w_glu': nrm((N_A_LAYERS, D_MODEL, 2 * D_MODEL), D_MODEL ** -0.5),
        'norm_kv': gain((D_MODEL,)),
        'w_k': nrm((D_MODEL, qk), D_MODEL ** -0.5),
        'w_v': nrm((D_MODEL, vd), D_MODEL ** -0.5),
        'w_q': nrm((N_B_LAYERS, D_MODEL, qk), D_MODEL ** -0.5),
        'lambda_q1': nrm((N_B_LAYERS, HEAD_DIM), 0.1),
        'lambda_k1': nrm((N_B_LAYERS, HEAD_DIM), 0.1),
        'lambda_q2': nrm((N_B_LAYERS, HEAD_DIM), 0.1),
        'lambda_k2': nrm((N_B_LAYERS, HEAD_DIM), 0.1),
        'norm_sub': gain((N_B_LAYERS, V_DIM)),
        'w_o': nrm((N_B_LAYERS, vd, D_MODEL), vd ** -0.5),
        'norm_ffn': gain((DEPTH, D_MODEL)),
        'w_up': nrm((DEPTH, D_MODEL, 2 * D_FF), D_MODEL ** -0.5),
        'conv_w': nrm((DEPTH, CONV_W, 2 * D_FF), CONV_W ** -0.5),
        'conv_b': nrm((DEPTH, 2 * D_FF), 0.01),
        'w_down': nrm((DEPTH, D_FF, D_MODEL), D_FF ** -0.5),
        'norm_ple': gain((DEPTH, D_MODEL)),
        'w_ple_gate': nrm((DEPTH, D_MODEL, D_MODEL), D_MODEL ** -0.5),
        'w_ple_proj': nrm((DEPTH, PLE_DIM, D_MODEL), PLE_DIM ** -0.5),
        'norm_final': gain((D_MODEL,)),
    }


def reference(x_prompt, x_sample, state_ssm_re, state_ssm_im, state_conv, cache_k, cache_v,
              p_prompt, p_sample, norm_mix, ssm_a_re, ssm_a_im, ssm_log_dt, ssm_b_re, ssm_b_im,
              ssm_c_re, ssm_c_im, ssm_d, w_glu, norm_kv, w_k, w_v, w_q, lambda_q1, lambda_k1,
              lambda_q2, lambda_k2, norm_sub, w_o, norm_ffn, w_up, conv_w, conv_b, w_down,
              norm_ple, w_ple_gate, w_ple_proj, norm_final):
    W = {
        'norm_mix': norm_mix, 'ssm_a_re': ssm_a_re, 'ssm_a_im': ssm_a_im,
        'ssm_log_dt': ssm_log_dt, 'ssm_b_re': ssm_b_re, 'ssm_b_im': ssm_b_im,
        'ssm_c_re': ssm_c_re, 'ssm_c_im': ssm_c_im, 'ssm_d': ssm_d, 'w_glu': w_glu,
        'norm_kv': norm_kv, 'w_k': w_k, 'w_v': w_v, 'w_q': w_q,
        'lambda_q1': lambda_q1, 'lambda_k1': lambda_k1, 'lambda_q2': lambda_q2,
        'lambda_k2': lambda_k2, 'norm_sub': norm_sub, 'w_o': w_o,
        'norm_ffn': norm_ffn, 'w_up': w_up, 'conv_w': conv_w, 'conv_b': conv_b,
        'w_down': w_down, 'norm_ple': norm_ple, 'w_ple_gate': w_ple_gate,
        'w_ple_proj': w_ple_proj, 'norm_final': norm_final,
    }
    bp = x_prompt.shape[0]
    z_ssm = jnp.zeros((N_A_LAYERS, bp, N_GROUPS, SSM_STATE), jnp.float32)
    z_conv = jnp.zeros((DEPTH, bp, CONV_W - 1, 2 * D_FF), x_prompt.dtype)
    y_p, sre_p, sim_p, conv_p, k_p, v_p = _run_group(
        x_prompt, p_prompt, 0, z_ssm, z_ssm, z_conv, None, None, W)
    y_s, sre_s, sim_s, conv_s, k_s, v_s = _run_group(
        x_sample, p_sample, cache_k.shape[1], state_ssm_re, state_ssm_im, state_conv,
        cache_k, cache_v, W)
    return (y_p, y_s, sre_p, sim_p, conv_p, k_p, v_p, sre_s, sim_s, conv_s, k_s, v_s)
```

```python
import functools
import math

import jax
import jax.numpy as jnp
import numpy as np
from jax import lax
from jax.experimental import pallas as pl
from jax.experimental.pallas import tpu as pltpu

F32 = jnp.float32
BF16 = jnp.bfloat16

EPS = 1e-6
SUBLN_EPS = 1e-5
CHUNK = 64
ROPE_THETA = 500000.0
SSM_GROUP = 16
SSM_STATE = 64
N_HEADS = 8
CONV_W = 3

LANES = 128
SUBLANES = 8
VMEM_LIMIT = 56 * 1024 * 1024
NEG_BIG = -1e30


def _cparams(n_axes):
    return pltpu.CompilerParams(dimension_semantics=("arbitrary",) * n_axes,
                                vmem_limit_bytes=VMEM_LIMIT)


def _rms(x, g, eps=EPS):
    return x * lax.rsqrt(jnp.mean(x * x, axis=-1, keepdims=True) + eps) * g


def _rows(start, n, stride):
    return pl.ds(start, n) if stride == 1 else pl.ds(start, n, stride=stride)


def _s5_kernel(h_ref, g_ref, wb_ref, wc_ref, lr_ref, li_ref, d_ref, st0_ref,
               y_ref, stout_ref, u_scr, s_scr, st_scr, *, nb, nq, tc):
    i = pl.program_id(1)
    rows = tc * nb
    nsl = s_scr.shape[0]
    half = nsl // 2
    n_in = u_scr.shape[0]
    spq = n_in // nq
    sub = 512 // LANES

    @pl.when(i == 0)
    def _():
        st_scr[...] = st0_ref[...]

    g = g_ref[...]
    for b in range(nb):
        u = _rms(h_ref[b], g)
        for m in range(n_in):
            u_scr[m, _rows(b, tc, nb), :] = u[:, m * LANES:(m + 1) * LANES]

    for m in range(n_in):
        r = jnp.dot(u_scr[m].astype(BF16), wb_ref[m], preferred_element_type=F32)
        j, ml = divmod(m, spq)
        for q in range(sub):
            s_scr[ml * sub + q, _rows(j, rows, nq), :] = r[:, q * LANES:(q + 1) * LANES]
            s_scr[half + ml * sub + q, _rows(j, rows, nq), :] = (
                r[:, 512 + q * LANES:512 + (q + 1) * LANES])

    pg = 4
    for p0 in range(0, half, pg):
        lr = [lr_ref[p0 + k] for k in range(pg)]
        li = [li_ref[p0 + k] for k in range(pg)]
        xr0 = tuple(st_scr[p0 + k] for k in range(pg))
        xi0 = tuple(st_scr[half + p0 + k] for k in range(pg))

        def body(t, carry, p0=p0, lr=lr, li=li):
            xr, xi = carry
            off = pl.multiple_of(t * SUBLANES, SUBLANES)
            nxr, nxi = [], []
            for k in range(pg):
                br = s_scr[p0 + k, pl.ds(off, SUBLANES), :]
                bi = s_scr[half + p0 + k, pl.ds(off, SUBLANES), :]
                r_ = lr[k] * xr[k] - li[k] * xi[k] + br
                i_ = lr[k] * xi[k] + li[k] * xr[k] + bi
                s_scr[p0 + k, pl.ds(off, SUBLANES), :] = r_
                s_scr[half + p0 + k, pl.ds(off, SUBLANES), :] = i_
                nxr.append(r_)
                nxi.append(i_)
            return tuple(nxr), tuple(nxi)

        xr, xi = lax.fori_loop(0, tc, body, (xr0, xi0), unroll=2)
        for k in range(pg):
            st_scr[p0 + k] = xr[k]
            st_scr[half + p0 + k] = xi[k]

    for m in range(n_in):
        j, ml = divmod(m, spq)
        pieces = ([s_scr[ml * sub + q, _rows(j, rows, nq), :] for q in range(sub)]
                  + [s_scr[half + ml * sub + q, _rows(j, rows, nq), :] for q in range(sub)])
        xs = jnp.concatenate(pieces, axis=-1).astype(BF16)
        ym = jnp.dot(xs, wc_ref[m], preferred_element_type=F32)
        u_scr[m] = ym + u_scr[m] * d_ref[:, m * LANES:(m + 1) * LANES]
        for b in range(nb):
            y_ref[b, :, m * LANES:(m + 1) * LANES] = u_scr[m, _rows(b, tc, nb), :]

    stout_ref[...] = st_scr[...]


def _s5_weights(a_re, a_im, log_dt, b_re, b_im, c_re, c_im):
    dt = jnp.exp(log_dt.astype(F32))[:, None]
    a_re = a_re.astype(F32)
    a_im = a_im.astype(F32)
    mag = jnp.exp(a_re * dt)
    lb_re = mag * jnp.cos(a_im * dt)
    lb_im = mag * jnp.sin(a_im * dt)
    den = a_re * a_re + a_im * a_im
    n_re = lb_re - 1.0
    k_re = (n_re * a_re + lb_im * a_im) / den
    k_im = (lb_im * a_re - n_re * a_im) / den
    b_re = b_re.astype(F32)
    b_im = b_im.astype(F32)
    bb_re = k_re[..., None] * b_re - k_im[..., None] * b_im
    bb_im = k_re[..., None] * b_im + k_im[..., None] * b_re
    n_g = a_re.shape[0]
    gps = LANES // SSM_GROUP
    n_in = n_g // gps
    eye = jnp.eye(gps, dtype=F32)

    def blockdiag_b(w):
        w = w.reshape(n_in, gps, SSM_STATE, SSM_GROUP)
        return jnp.einsum('mgpc,gh->mgchp', w, eye).reshape(n_in, LANES, gps * SSM_STATE)

    def blockdiag_c(w):
        w = w.reshape(n_in, gps, SSM_GROUP, SSM_STATE)
        return jnp.einsum('mgcp,gh->mgphc', w, eye).reshape(n_in, gps * SSM_STATE, LANES)

    wb = jnp.concatenate([blockdiag_b(bb_re), blockdiag_b(bb_im)], axis=-1).astype(BF16)
    wc = jnp.concatenate([blockdiag_c(c_re.astype(F32)), -blockdiag_c(c_im.astype(F32))],
                         axis=1).astype(BF16)
    return lb_re, lb_im, wb, wc


def _state_rows(x, nb, nq):
    n_seq = x.shape[0]
    spq = (x.shape[1] * SSM_GROUP // LANES) // nq
    sub = 512 // LANES
    x = x.reshape(n_seq // nb, nb, nq, spq, sub, LANES)
    x = x.transpose(0, 3, 4, 1, 2, 5)
    return x.reshape(n_seq // nb, spq * sub, nb * nq, LANES)


def _state_unrows(x, nb, nq, n_g):
    ng, half = x.shape[0], x.shape[1]
    sub = 512 // LANES
    spq = half // sub
    x = x.reshape(ng, spq, sub, nb, nq, LANES).transpose(0, 3, 4, 1, 2, 5)
    return x.reshape(ng * nb, n_g, SSM_STATE)


def _s5_layer(h, g, s_re0, s_im0, d_skip, lb_re, lb_im, wb, wc, *, nb, nq, tc):
    n_seq, L, D = h.shape
    n_g = lb_re.shape[0]
    ng = n_seq // nb
    st0 = jnp.concatenate([_state_rows(s_re0.astype(F32), nb, nq),
                           _state_rows(s_im0.astype(F32), nb, nq)], axis=1)
    nsl = st0.shape[1]
    half = nsl // 2
    lr = jnp.broadcast_to(_state_rows(lb_re[None], 1, nq)[0][:, None], (half, nb, nq, LANES))
    li = jnp.broadcast_to(_state_rows(lb_im[None], 1, nq)[0][:, None], (half, nb, nq, LANES))
    lr = lr.reshape(half, nb * nq, LANES)
    li = li.reshape(half, nb * nq, LANES)
    n_in = D // LANES
    kern = functools.partial(_s5_kernel, nb=nb, nq=nq, tc=tc)
    full = lambda shape: pl.BlockSpec(shape, lambda gi, i: (0,) * len(shape))
    y, stout = pl.pallas_call(
        kern,
        grid=(ng, L // tc),
        in_specs=[
            pl.BlockSpec((nb, tc, D), lambda gi, i: (gi, i, 0)),
            full((1, D)),
            full(wb.shape),
            full(wc.shape),
            full(lr.shape),
            full(li.shape),
            full((1, D)),
            pl.BlockSpec((None, nsl, SUBLANES, LANES), lambda gi, i: (gi, 0, 0, 0)),
        ],
        out_specs=[
            pl.BlockSpec((nb, tc, D), lambda gi, i: (gi, i, 0)),
            pl.BlockSpec((None, nsl, SUBLANES, LANES), lambda gi, i: (gi, 0, 0, 0)),
        ],
        out_shape=[jax.ShapeDtypeStruct((n_seq, L, D), F32),
                   jax.ShapeDtypeStruct((ng, nsl, SUBLANES, LANES), F32)],
        scratch_shapes=[pltpu.VMEM((n_in, tc * nb, LANES), F32),
                        pltpu.VMEM((nsl, tc * SUBLANES, LANES), F32),
                        pltpu.VMEM((nsl, SUBLANES, LANES), F32)],
        compiler_params=_cparams(2),
        name="s5_mixer",
    )(h, g.reshape(1, D), wb, wc, lr, li, d_skip.reshape(1, D).astype(F32), st0)
    s_re = _state_unrows(stout[:, :half], nb, nq, n_g)
    s_im = _state_unrows(stout[:, half:], nb, nq, n_g)
    return y, s_re, s_im


def _glu_kernel(y_ref, h_ref, w1_ref, w2_ref, out_ref, gy_scr):
    @pl.when(pl.program_id(1) == 0)
    def _():
        gy_scr[...] = jax.nn.gelu(y_ref[...], approximate=True).astype(BF16)

    gy = gy_scr[...]
    z1 = jnp.dot(gy, w1_ref[...], preferred_element_type=F32)
    z2 = jnp.dot(gy, w2_ref[...], preferred_element_type=F32)
    out_ref[...] = h_ref[...] + z1 * jax.nn.sigmoid(z2)


def _glu_layer(y, h, w_glu_bf, *, tm, tn=512):
    M, D = h.shape
    nn = D // tn
    return pl.pallas_call(
        _glu_kernel,
        grid=(M // tm, nn),
        in_specs=[
            pl.BlockSpec((tm, D), lambda i, n: (i, 0)),
            pl.BlockSpec((tm, tn), lambda i, n: (i, n)),
            pl.BlockSpec((D, tn), lambda i, n: (0, n)),
            pl.BlockSpec((D, tn), lambda i, n: (0, nn + n)),
        ],
        out_specs=pl.BlockSpec((tm, tn), lambda i, n: (i, n)),
        out_shape=jax.ShapeDtypeStruct((M, D), F32),
        scratch_shapes=[pltpu.VMEM((tm, D), BF16)],
        compiler_params=_cparams(2),
        name="glu",
    )(y, h, w_glu_bf, w_glu_bf)


def _ffn_kernel(h_ref, g_ref, wa_ref, wg_ref, cwa_ref, cwg_ref, cba_ref, cbg_ref, wd_ref,
                c0a_ref, c0g_ref, out_ref, coa_ref, cog_ref,
                xn_scr, za_scr, zg_scr, cara_scr, carg_scr, *, nbt, seg, tpb):
    i = pl.program_id(0)
    f = pl.program_id(1)
    pad = SUBLANES
    ctx = CONV_W - 1

    @pl.when(f == 0)
    def _():
        x = h_ref[...]
        xn_scr[...] = _rms(x, g_ref[...]).astype(BF16)
        out_ref[...] = x

    xn = xn_scr[...]

    def conv_half(w_ref, cw_ref, cb_ref, c0_ref, co_ref, z_scr, car_scr):
        z = jnp.dot(xn, w_ref[...], preferred_element_type=F32)
        fc = z.shape[-1]
        z_scr[:, pad:, :] = z.reshape(nbt, seg, fc)
        if tpb == 1:
            z_scr[:, pad - ctx:pad, :] = c0_ref[...]
        else:
            first = (i % tpb) == 0

            @pl.when(first)
            def _():
                z_scr[:, pad - ctx:pad, :] = c0_ref[...]

            @pl.when(jnp.logical_not(first))
            def _():
                z_scr[0, pad - ctx:pad, :] = car_scr[f, pad - ctx:pad, :]

            car_scr[f, pad - ctx:pad, :] = z_scr[0, pad + seg - ctx:pad + seg, :]
        cw = cw_ref[...]
        c = cb_ref[...]
        for k in range(CONV_W):
            c = c + cw[k:k + 1] * z_scr[:, pad - ctx + k:pad - ctx + k + seg, :]
        co_ref[...] = z_scr[:, pad + seg - ctx:pad + seg, :]
        return c.reshape(nbt * seg, fc)

    ca = conv_half(wa_ref, cwa_ref, cba_ref, c0a_ref, coa_ref, za_scr, cara_scr)
    cg = conv_half(wg_ref, cwg_ref, cbg_ref, c0g_ref, cog_ref, zg_scr, carg_scr)
    act = (jax.nn.silu(cg) * ca).astype(BF16)
    out_ref[...] += jnp.dot(act, wd_ref[...], preferred_element_type=F32)


def _ffn_layer(h, conv0, g, w_up_bf, conv_w, conv_b, w_down_bf, *, n_seq, nbt, seg, fc=512):
    M, D = h.shape
    dff = w_down_bf.shape[0]
    nf = dff // fc
    L = M // n_seq
    tm = nbt * seg
    tpb = L // seg
    cwa, cwg = conv_w[:, :dff].astype(F32), conv_w[:, dff:].astype(F32)
    cba, cbg = conv_b[:dff].reshape(1, dff).astype(F32), conv_b[dff:].reshape(1, dff).astype(F32)
    c0a, c0g = conv0[:, :, :dff].astype(F32), conv0[:, :, dff:].astype(F32)
    ctx = CONV_W - 1
    kern = functools.partial(_ffn_kernel, nbt=nbt, seg=seg, tpb=tpb)
    cmap = lambda i, f: (i // tpb, 0, f)
    out, coa, cog = pl.pallas_call(
        kern,
        grid=(M // tm, nf),
        in_specs=[
            pl.BlockSpec((tm, D), lambda i, f: (i, 0)),
            pl.BlockSpec((1, D), lambda i, f: (0, 0)),
            pl.BlockSpec((D, fc), lambda i, f: (0, f)),
            pl.BlockSpec((D, fc), lambda i, f: (0, nf + f)),
            pl.BlockSpec((CONV_W, fc), lambda i, f: (0, f)),
            pl.BlockSpec((CONV_W, fc), lambda i, f: (0, f)),
            pl.BlockSpec((1, fc), lambda i, f: (0, f)),
            pl.BlockSpec((1, fc), lambda i, f: (0, f)),
            pl.BlockSpec((fc, D), lambda i, f: (f, 0)),
            pl.BlockSpec((nbt, ctx, fc), cmap),
            pl.BlockSpec((nbt, ctx, fc), cmap),
        ],
        out_specs=[
            pl.BlockSpec((tm, D), lambda i, f: (i, 0)),
            pl.BlockSpec((None, nbt, ctx, fc), lambda i, f: (i, 0, 0, f)),
            pl.BlockSpec((None, nbt, ctx, fc), lambda i, f: (i, 0, 0, f)),
        ],
        out_shape=[jax.ShapeDtypeStruct((M, D), F32),
                   jax.ShapeDtypeStruct((M // tm, nbt, ctx, dff), F32),
                   jax.ShapeDtypeStruct((M // tm, nbt, ctx, dff), F32)],
        scratch_shapes=[pltpu.VMEM((tm, D), BF16),
                        pltpu.VMEM((nbt, seg + SUBLANES, fc), F32),
                        pltpu.VMEM((nbt, seg + SUBLANES, fc), F32),
                        pltpu.VMEM((nf, SUBLANES, fc), F32),
                        pltpu.VMEM((nf, SUBLANES, fc), F32)],
        compiler_params=_cparams(2),
        name="conv_ffn",
    )(h, g.reshape(1, D), w_up_bf, w_up_bf, cwa, cwg, cba, cbg, w_down_bf, c0a, c0g)
    co = jnp.concatenate([coa, cog], axis=-1)[tpb - 1::tpb]
    return out, co.reshape(n_seq, ctx, 2 * dff)


def _ple_kernel(h_ref, p_ref, g_ref, wg_ref, wp_ref, gf_ref, out_ref, *, final, tn):
    x = h_ref[...]
    n = _rms(x, g_ref[...]).astype(BF16)
    pb = p_ref[...].astype(BF16)
    for c in range(x.shape[-1] // tn):
        sl = slice(c * tn, (c + 1) * tn)
        gate = jax.nn.sigmoid(jnp.dot(n, wg_ref[:, sl], preferred_element_type=F32))
        pr = jnp.dot(pb, wp_ref[:, sl], preferred_element_type=F32)
        out_ref[:, sl] = x[:, sl] + pr * gate
    if final:
        out_ref[...] = _rms(out_ref[...], gf_ref[...])


def _ple_layer(h, p, g, w_gate_bf, w_proj_bf, g_final, *, final, tm, tn=512):
    M, D = h.shape
    pd = p.shape[-1]
    kern = functools.partial(_ple_kernel, final=final, tn=tn)
    return pl.pallas_call(
        kern,
        grid=(M // tm,),
        in_specs=[
            pl.BlockSpec((tm, D), lambda i: (i, 0)),
            pl.BlockSpec((tm, pd), lambda i: (i, 0)),
            pl.BlockSpec((1, D), lambda i: (0, 0)),
            pl.BlockSpec((D, D), lambda i: (0, 0)),
            pl.BlockSpec((pd, D), lambda i: (0, 0)),
            pl.BlockSpec((1, D), lambda i: (0, 0)),
        ],
        out_specs=pl.BlockSpec((tm, D), lambda i: (i, 0)),
        out_shape=jax.ShapeDtypeStruct((M, D), F32),
        compiler_params=_cparams(1),
        name="ple",
    )(h, p, g.reshape(1, D), w_gate_bf, w_proj_bf, g_final.reshape(1, D))


def _rope_tables(pos):
    rot = LANES // 4
    hr = rot // 2
    inv = ROPE_THETA ** (-jnp.arange(hr, dtype=F32) * 2.0 / rot)
    ang = pos.astype(F32)[:, None] * inv[None, :]
    cos, sin = jnp.cos(ang), jnp.sin(ang)
    n = pos.shape[0]
    one = jnp.ones((n, LANES - rot), F32)
    zero_h = jnp.zeros((n, hr), F32)
    zero_r = jnp.zeros((n, LANES - rot), F32)
    c = jnp.concatenate([cos, cos, one], axis=-1)
    s1 = jnp.concatenate([-sin, zero_h, zero_r], axis=-1)
    s2 = jnp.concatenate([zero_h, sin, zero_r], axis=-1)
    return c, s1, s2


def _proj_kernel(*refs, rope, scale, f32_out, bf_out, tn):
    refs = list(refs)
    h_ref, g_ref, w_ref = refs[:3]
    refs = refs[3:]
    if rope:
        c_ref, s1_ref, s2_ref = refs[:3]
        refs = refs[3:]
    of_ref = refs.pop(0) if f32_out else None
    ob_ref = refs.pop(0) if bf_out else None
    n = _rms(h_ref[...], g_ref[...]).astype(BF16)
    hr = LANES // 8
    for c in range(w_ref.shape[-1] // tn):
        sl = slice(c * tn, (c + 1) * tn)
        r = jnp.dot(n, w_ref[:, sl], preferred_element_type=F32)
        if rope:
            cs, s1, s2 = c_ref[...], s1_ref[...], s2_ref[...]
            parts = []
            for s in range(tn // LANES):
                xb = r[:, s * LANES:(s + 1) * LANES]
                parts.append(xb * cs + pltpu.roll(xb, LANES - hr, 1) * s1 + pltpu.roll(xb, hr, 1) * s2)
            r = jnp.concatenate(parts, axis=-1)
        if f32_out:
            of_ref[:, sl] = r
        if bf_out:
            ob_ref[:, sl] = (r * scale).astype(BF16) if scale != 1.0 else r.astype(BF16)


def _proj_layer(h, g, w_bf, tabs, *, scale=1.0, f32_out, bf_out, tm, tn=512):
    M, D = h.shape
    N = w_bf.shape[-1]
    rope = tabs is not None
    kern = functools.partial(_proj_kernel, rope=rope, scale=scale, f32_out=f32_out,
                             bf_out=bf_out, tn=tn)
    in_specs = [pl.BlockSpec((tm, D), lambda i: (i, 0)),
                pl.BlockSpec((1, D), lambda i: (0, 0)),
                pl.BlockSpec((D, N), lambda i: (0, 0))]
    args = [h, g.reshape(1, D), w_bf]
    if rope:
        in_specs += [pl.BlockSpec((tm, LANES), lambda i: (i, 0))] * 3
        args += list(tabs)
    out_specs, out_shape = [], []
    if f32_out:
        out_specs.append(pl.BlockSpec((tm, N), lambda i: (i, 0)))
        out_shape.append(jax.ShapeDtypeStruct((M, N), F32))
    if bf_out:
        out_specs.append(pl.BlockSpec((tm, N), lambda i: (i, 0)))
        out_shape.append(jax.ShapeDtypeStruct((M, N), BF16))
    return pl.pallas_call(
        kern, grid=(M // tm,), in_specs=in_specs, out_specs=out_specs, out_shape=out_shape,
        compiler_params=_cparams(1), name="norm_proj",
    )(*args)


def _softmax_tile(qs, k, v, ms, ls, accs, mask):
    new_m, new_l = [], []
    for c in range(2):
        s = lax.dot_general(qs[c], k[:, c * LANES:(c + 1) * LANES], (((1,), (1,)), ((), ())),
                            preferred_element_type=F32)
        if mask is not None:
            s = jnp.where(mask, s, NEG_BIG)
        mn = jnp.maximum(ms[c], jnp.max(s, axis=-1, keepdims=True))
        a = jnp.exp(ms[c] - mn)
        p = jnp.exp(s - mn)
        new_l.append(a * ls[c] + jnp.sum(p, axis=-1, keepdims=True))
        accs[c][...] = a * accs[c][...] + jnp.dot(p.astype(BF16), v, preferred_element_type=F32)
        new_m.append(mn)
    return tuple(new_m), tuple(new_l)


def _attn_finish(ls, accs, lam_ref, gs_ref, o_ref, lam_init):
    o = accs[0][...] / ls[0] - lam_ref[...] * (accs[1][...] / ls[1])
    o = _rms(o, gs_ref[...], SUBLN_EPS) * (1.0 - lam_init)
    o_ref[...] = o.astype(o_ref.dtype)


def _attn_prompt_kernel(q_ref, k_ref, v_ref, lam_ref, gs_ref, o_ref, acc0, acc1, *, tq, lam_init):
    qi = pl.program_id(2)
    q = q_ref[...]
    qs = (q[:, :LANES], q[:, LANES:])
    accs = (acc0, acc1)
    acc0[...] = jnp.zeros_like(acc0)
    acc1[...] = jnp.zeros_like(acc1)
    m0 = jnp.full((tq, 1), NEG_BIG, F32)
    l0 = jnp.zeros((tq, 1), F32)

    def tile(ki, carry, mask):
        ms, ls = carry
        off = pl.multiple_of(ki * tq, tq)
        k = k_ref[pl.ds(off, tq), :]
        v = v_ref[pl.ds(off, tq), :]
        return _softmax_tile(qs, k, v, ms, ls, accs, mask)

    carry = lax.fori_loop(0, qi, lambda ki, c: tile(ki, c, None), ((m0, m0), (l0, l0)))
    rc = lax.broadcasted_iota(jnp.int32, (tq, tq), 0) // CHUNK
    cc = lax.broadcasted_iota(jnp.int32, (tq, tq), 1) // CHUNK
    ms, ls = tile(qi, carry, cc <= rc)
    _attn_finish(ls, accs, lam_ref, gs_ref, o_ref, lam_init)


def _attn_prompt(q, k, v, lam_row, g_sub, *, lam_init, tq=512):
    B, L, D = q.shape
    vd = D // N_HEADS
    kern = functools.partial(_attn_prompt_kernel, tq=tq, lam_init=lam_init)
    return pl.pallas_call(
        kern,
        grid=(B, N_HEADS, L // tq),
        in_specs=[
            pl.BlockSpec((None, tq, vd), lambda b, h, i: (b, i, h)),
            pl.BlockSpec((None, L, vd), lambda b, h, i: (b, 0, h)),
            pl.BlockSpec((None, L, vd), lambda b, h, i: (b, 0, h)),
            pl.BlockSpec((1, vd), lambda b, h, i: (0, 0)),
            pl.BlockSpec((1, vd), lambda b, h, i: (0, 0)),
        ],
        out_specs=pl.BlockSpec((None, tq, vd), lambda b, h, i: (b, i, h)),
        out_shape=jax.ShapeDtypeStruct((B, L, D), BF16),
        scratch_shapes=[pltpu.VMEM((tq, vd), F32), pltpu.VMEM((tq, vd), F32)],
        compiler_params=_cparams(3),
        name="diff_attn_prompt",
    )(q, k, v, lam_row, g_sub.reshape(1, vd))


def _attn_cached_kernel(q_ref, kc_ref, vc_ref, kn_ref, vn_ref, lam_ref, gs_ref, o_ref, acc0, acc1,
                        *, past, lam_init):
    q = q_ref[...]
    sq = q.shape[0]
    qs = (q[:, :LANES], q[:, LANES:])
    accs = (acc0, acc1)
    acc0[...] = jnp.zeros_like(acc0)
    acc1[...] = jnp.zeros_like(acc1)
    m0 = jnp.full((sq, 1), NEG_BIG, F32)
    l0 = jnp.zeros((sq, 1), F32)
    q_cid = (past + lax.broadcasted_iota(jnp.int32, (sq, past), 0)) // CHUNK
    mask_c = lax.broadcasted_iota(jnp.int32, (sq, past), 1) // CHUNK <= q_cid
    carry = _softmax_tile(qs, kc_ref[...].astype(BF16), vc_ref[...].astype(BF16),
                          (m0, m0), (l0, l0), accs, mask_c)
    rc = (past + lax.broadcasted_iota(jnp.int32, (sq, sq), 0)) // CHUNK
    cc = (past + lax.broadcasted_iota(jnp.int32, (sq, sq), 1)) // CHUNK
    ms, ls = _softmax_tile(qs, kn_ref[...], vn_ref[...], carry[0], carry[1], accs, cc <= rc)
    _attn_finish(ls, accs, lam_ref, gs_ref, o_ref, lam_init)


def _attn_cached(q, k_cache, v_cache, k_new, v_new, lam_row, g_sub, *, lam_init):
    B, sq, D = q.shape
    past = k_cache.shape[1]
    vd = D // N_HEADS
    kern = functools.partial(_attn_cached_kernel, past=past, lam_init=lam_init)
    blk = lambda n: pl.BlockSpec((None, n, vd), lambda b, h: (b, 0, h))
    return pl.pallas_call(
        kern,
        grid=(B, N_HEADS),
        in_specs=[blk(sq), blk(past), blk(past), blk(sq), blk(sq),
                  pl.BlockSpec((1, vd), lambda b, h: (0, 0)),
                  pl.BlockSpec((1, vd), lambda b, h: (0, 0))],
        out_specs=blk(sq),
        out_shape=jax.ShapeDtypeStruct((B, sq, D), BF16),
        scratch_shapes=[pltpu.VMEM((sq, vd), F32), pltpu.VMEM((sq, vd), F32)],
        compiler_params=_cparams(2),
        name="diff_attn_cached",
    )(q, k_cache, v_cache, k_new, v_new, lam_row, g_sub.reshape(1, vd))


def _wo_kernel(o_ref, h_ref, w_ref, out_ref, *, tn):
    o = o_ref[...]
    for c in range(w_ref.shape[-1] // tn):
        sl = slice(c * tn, (c + 1) * tn)
        out_ref[:, sl] = h_ref[:, sl] + jnp.dot(o, w_ref[:, sl], preferred_element_type=F32)


def _wo_layer(o, h, w_bf, *, tm, tn=512):
    M, D = h.shape
    return pl.pallas_call(
        functools.partial(_wo_kernel, tn=tn),
        grid=(M // tm,),
        in_specs=[pl.BlockSpec((tm, D), lambda i: (i, 0)),
                  pl.BlockSpec((tm, D), lambda i: (i, 0)),
                  pl.BlockSpec((D, D), lambda i: (0, 0))],
        out_specs=pl.BlockSpec((tm, D), lambda i: (i, 0)),
        out_shape=jax.ShapeDtypeStruct((M, D), F32),
        compiler_params=_cparams(1),
        name="attn_out_proj",
    )(o, h, w_bf)


def _run_group(x, p, pos0, ssm_re0, ssm_im0, conv0, k_cache, v_cache, W, cfg):
    bt, L, D = x.shape
    M = bt * L
    depth = W['norm_mix'].shape[0]
    n_a = W['ssm_a_re'].shape[0]
    tm = cfg['tm']
    pos = pos0 + jnp.arange(L, dtype=jnp.int32)
    h = x.reshape(M, D)
    ssm_re, ssm_im, conv_new = [], [], []
    k_f32 = v_f32 = k_bf = v_bf = None
    for i in range(depth):
        if i < n_a:
            y, sr, si = _s5_layer(h.reshape(bt, L, D), W['norm_mix'][i], ssm_re0[i], ssm_im0[i],
                                  W['ssm_d'][i], *W['s5'][i],
                                  nb=cfg['s5_nb'], nq=cfg['s5_nq'], tc=cfg['s5_tc'])
            ssm_re.append(sr)
            ssm_im.append(si)
            h = _glu_layer(y.reshape(M, D), h, W['w_glu'][i], tm=tm)
        else:
            j = i - n_a
            tabs = tuple(jnp.tile(t, (bt, 1)) for t in _rope_tables(pos))
            if i == n_a:
                k_f32, k_bf = _proj_layer(h, W['norm_kv'], W['w_k'], tabs, f32_out=True,
                                          bf_out=True, tm=tm)
                v_f32, v_bf = _proj_layer(h, W['norm_kv'], W['w_v'], None, f32_out=True,
                                          bf_out=True, tm=tm)
            (q_bf,) = _proj_layer(h, W['norm_mix'][i], W['w_q'][j], tabs, scale=LANES ** -0.5,
                                  f32_out=False, bf_out=True, tm=tm)
            lam_init = 0.8 - 0.6 * math.exp(-0.3 * i)
            lam = (jnp.exp(jnp.sum(W['lambda_q1'][j].astype(F32) * W['lambda_k1'][j].astype(F32)))
                   - jnp.exp(jnp.sum(W['lambda_q2'][j].astype(F32) * W['lambda_k2'][j].astype(F32)))
                   + lam_init)
            lam_row = jnp.full((1, D // N_HEADS), lam, F32)
            q3 = q_bf.reshape(bt, L, D)
            if k_cache is None:
                o = _attn_prompt(q3, k_bf.reshape(bt, L, D), v_bf.reshape(bt, L, D), lam_row,
                                 W['norm_sub'][j], lam_init=lam_init)
            else:
                past = k_cache.shape[1]
                o = _attn_cached(q3, k_cache.reshape(bt, past, D), v_cache.reshape(bt, past, D),
                                 k_bf.reshape(bt, L, D), v_bf.reshape(bt, L, D), lam_row,
                                 W['norm_sub'][j], lam_init=lam_init)
            h = _wo_layer(o.reshape(M, D), h, W['w_o'][j], tm=tm)
        h, cs = _ffn_layer(h, conv0[i], W['norm_ffn'][i], W['w_up'][i], W['conv_w'][i],
                           W['conv_b'][i], W['w_down'][i], n_seq=bt, nbt=cfg['ffn_nbt'],
                           seg=cfg['ffn_seg'])
        conv_new.append(cs)
        h = _ple_layer(h, p[i].reshape(M, -1), W['norm_ple'][i], W['w_ple_gate'][i],
                       W['w_ple_proj'][i], W['norm_final'], final=(i == depth - 1), tm=tm)
    vd = D // N_HEADS
    return (h.reshape(bt, L, D), jnp.stack(ssm_re), jnp.stack(ssm_im), jnp.stack(conv_new),
            k_f32.reshape(bt, L, N_HEADS, vd), v_f32.reshape(bt, L, N_HEADS, vd))


def _group_cfg(bt, L):
    if L >= 1024:
        return dict(tm=512, s5_nb=bt, s5_nq=SUBLANES // bt, s5_tc=128, ffn_nbt=1, ffn_seg=512)
    return dict(tm=bt * L, s5_nb=SUBLANES, s5_nq=1, s5_tc=L, ffn_nbt=bt, ffn_seg=L)


def kernel(x_prompt, x_sample, state_ssm_re, state_ssm_im, state_conv, cache_k, cache_v, p_prompt, p_sample, norm_mix, ssm_a_re, ssm_a_im, ssm_log_dt, ssm_b_re, ssm_b_im, ssm_c_re, ssm_c_im, ssm_d, w_glu, norm_kv, w_k, w_v, w_q, lambda_q1, lambda_k1, lambda_q2, lambda_k2, norm_sub, w_o, norm_ffn, w_up, conv_w, conv_b, w_down, norm_ple, w_ple_gate, w_ple_proj, norm_final):
    n_a = ssm_a_re.shape[0]
    depth = norm_mix.shape[0]
    W = {
        'norm_mix': norm_mix, 'ssm_d': ssm_d, 'norm_kv': norm_kv,
        'lambda_q1': lambda_q1, 'lambda_k1': lambda_k1, 'lambda_q2': lambda_q2,
        'lambda_k2': lambda_k2, 'norm_sub': norm_sub, 'norm_ffn': norm_ffn, 'conv_w': conv_w,
        'conv_b': conv_b, 'norm_ple': norm_ple, 'norm_final': norm_final,
        'ssm_a_re': ssm_a_re,
        'w_glu': w_glu.astype(BF16), 'w_k': w_k.astype(BF16), 'w_v': w_v.astype(BF16),
        'w_q': w_q.astype(BF16), 'w_o': w_o.astype(BF16), 'w_up': w_up.astype(BF16),
        'w_down': w_down.astype(BF16), 'w_ple_gate': w_ple_gate.astype(BF16),
        'w_ple_proj': w_ple_proj.astype(BF16),
        's5': [_s5_weights(ssm_a_re[i], ssm_a_im[i], ssm_log_dt[i], ssm_b_re[i], ssm_b_im[i],
                           ssm_c_re[i], ssm_c_im[i]) for i in range(n_a)],
    }
    bp, lp, _ = x_prompt.shape
    bs, ls, _ = x_sample.shape
    n_g = ssm_a_re.shape[1]
    z_ssm = jnp.zeros((n_a, bp, n_g, SSM_STATE), F32)
    z_conv = jnp.zeros((depth, bp, CONV_W - 1, conv_w.shape[-1]), F32)
    y_p, sre_p, sim_p, conv_p, k_p, v_p = _run_group(
        x_prompt, p_prompt, 0, z_ssm, z_ssm, z_conv, None, None, W, _group_cfg(bp, lp))
    y_s, sre_s, sim_s, conv_s, k_s, v_s = _run_group(
        x_sample, p_sample, cache_k.shape[1], state_ssm_re, state_ssm_im, state_conv,
        cache_k, cache_v, W, _group_cfg(bs, ls))
    return (y_p, y_s, sre_p, sim_p, conv_p, k_p, v_p, sre_s, sim_s, conv_s, k_s, v_s)
```

```python
import functools
import math

import jax
import jax.numpy as jnp
import numpy as np
from jax import lax
from jax.experimental import pallas as pl
from jax.experimental.pallas import tpu as pltpu

F32 = jnp.float32
BF16 = jnp.bfloat16

EPS = 1e-6
SUBLN_EPS = 1e-5
CHUNK = 64
ROPE_THETA = 500000.0
SSM_GROUP = 16
SSM_STATE = 64
N_HEADS = 8
CONV_W = 3

LANES = 128
SUBLANES = 8
VMEM_LIMIT = 56 * 1024 * 1024
NEG_BIG = -1e30


def _cparams(n_axes):
    return pltpu.CompilerParams(dimension_semantics=("arbitrary",) * n_axes,
                                vmem_limit_bytes=VMEM_LIMIT)


def _rms(x, g, eps=EPS):
    return x * lax.rsqrt(jnp.mean(x * x, axis=-1, keepdims=True) + eps) * g


def _rows(start, n, stride):
    return pl.ds(start, n) if stride == 1 else pl.ds(start, n, stride=stride)


def _s5_kernel(h_ref, g_ref, wb_ref, wc_ref, lr_ref, li_ref, d_ref, st0_ref,
               y_ref, stout_ref, u_scr, s_scr, st_scr, *, nb, nq, tc):
    i = pl.program_id(1)
    rows = tc * nb
    nsl = s_scr.shape[0]
    half = nsl // 2
    n_in = u_scr.shape[0]
    spq = n_in // nq
    sub = 512 // LANES

    @pl.when(i == 0)
    def _():
        st_scr[...] = st0_ref[...]

    g = g_ref[...]
    for b in range(nb):
        u = _rms(h_ref[b], g)
        for m in range(n_in):
            u_scr[m, _rows(b, tc, nb), :] = u[:, m * LANES:(m + 1) * LANES]

    for m in range(n_in):
        r = jnp.dot(u_scr[m].astype(BF16), wb_ref[m], preferred_element_type=F32)
        j, ml = divmod(m, spq)
        for q in range(sub):
            s_scr[ml * sub + q, _rows(j, rows, nq), :] = r[:, q * LANES:(q + 1) * LANES]
            s_scr[half + ml * sub + q, _rows(j, rows, nq), :] = (
                r[:, 512 + q * LANES:512 + (q + 1) * LANES])

    pg = 4
    for p0 in range(0, half, pg):
        lr = [lr_ref[p0 + k] for k in range(pg)]
        li = [li_ref[p0 + k] for k in range(pg)]
        xr0 = tuple(st_scr[p0 + k] for k in range(pg))
        xi0 = tuple(st_scr[half + p0 + k] for k in range(pg))

        def body(t, carry, p0=p0, lr=lr, li=li):
            xr, xi = carry
            off = pl.multiple_of(t * SUBLANES, SUBLANES)
            nxr, nxi = [], []
            for k in range(pg):
                br = s_scr[p0 + k, pl.ds(off, SUBLANES), :]
                bi = s_scr[half + p0 + k, pl.ds(off, SUBLANES), :]
                r_ = lr[k] * xr[k] - li[k] * xi[k] + br
                i_ = lr[k] * xi[k] + li[k] * xr[k] + bi
                s_scr[p0 + k, pl.ds(off, SUBLANES), :] = r_
                s_scr[half + p0 + k, pl.ds(off, SUBLANES), :] = i_
                nxr.append(r_)
                nxi.append(i_)
            return tuple(nxr), tuple(nxi)

        xr, xi = lax.fori_loop(0, tc, body, (xr0, xi0), unroll=2)
        for k in range(pg):
            st_scr[p0 + k] = xr[k]
            st_scr[half + p0 + k] = xi[k]

    for m in range(n_in):
        j, ml = divmod(m, spq)
        pieces = ([s_scr[ml * sub + q, _rows(j, rows, nq), :] for q in range(sub)]
                  + [s_scr[half + ml * sub + q, _rows(j, rows, nq), :] for q in range(sub)])
        xs = jnp.concatenate(pieces, axis=-1).astype(BF16)
        ym = jnp.dot(xs, wc_ref[m], preferred_element_type=F32)
        u_scr[m] = ym + u_scr[m] * d_ref[:, m * LANES:(m + 1) * LANES]
        for b in range(nb):
            y_ref[b, :, m * LANES:(m + 1) * LANES] = u_scr[m, _rows(b, tc, nb), :]

    stout_ref[...] = st_scr[...]


def _s5_weights(a_re, a_im, log_dt, b_re, b_im, c_re, c_im):
    dt = jnp.exp(log_dt.astype(F32))[:, None]
    a_re = a_re.astype(F32)
    a_im = a_im.astype(F32)
    mag = jnp.exp(a_re * dt)
    lb_re = mag * jnp.cos(a_im * dt)
    lb_im = mag * jnp.sin(a_im * dt)
    den = a_re * a_re + a_im * a_im
    n_re = lb_re - 1.0
    k_re = (n_re * a_re + lb_im * a_im) / den
    k_im = (lb_im * a_re - n_re * a_im) / den
    b_re = b_re.astype(F32)
    b_im = b_im.astype(F32)
    bb_re = k_re[..., None] * b_re - k_im[..., None] * b_im
    bb_im = k_re[..., None] * b_im + k_im[..., None] * b_re
    n_g = a_re.shape[0]
    gps = LANES // SSM_GROUP
    n_in = n_g // gps
    eye = jnp.eye(gps, dtype=F32)

    def blockdiag_b(w):
        w = w.reshape(n_in, gps, SSM_STATE, SSM_GROUP)
        return jnp.einsum('mgpc,gh->mgchp', w, eye).reshape(n_in, LANES, gps * SSM_STATE)

    def blockdiag_c(w):
        w = w.reshape(n_in, gps, SSM_GROUP, SSM_STATE)
        return jnp.einsum('mgcp,gh->mgphc', w, eye).reshape(n_in, gps * SSM_STATE, LANES)

    wb = jnp.concatenate([blockdiag_b(bb_re), blockdiag_b(bb_im)], axis=-1).astype(BF16)
    wc = jnp.concatenate([blockdiag_c(c_re.astype(F32)), -blockdiag_c(c_im.astype(F32))],
                         axis=1).astype(BF16)
    return lb_re, lb_im, wb, wc


def _state_rows(x, nb, nq):
    n_seq = x.shape[0]
    spq = (x.shape[1] * SSM_GROUP // LANES) // nq
    sub = 512 // LANES
    x = x.reshape(n_seq // nb, nb, nq, spq, sub, LANES)
    x = x.transpose(0, 3, 4, 1, 2, 5)
    return x.reshape(n_seq // nb, spq * sub, nb * nq, LANES)


def _state_unrows(x, nb, nq, n_g):
    ng, half = x.shape[0], x.shape[1]
    sub = 512 // LANES
    spq = half // sub
    x = x.reshape(ng, spq, sub, nb, nq, LANES).transpose(0, 3, 4, 1, 2, 5)
    return x.reshape(ng * nb, n_g, SSM_STATE)


def _s5_layer(h, g, s_re0, s_im0, d_skip, lb_re, lb_im, wb, wc, *, nb, nq, tc):
    n_seq, L, D = h.shape
    n_g = lb_re.shape[0]
    ng = n_seq // nb
    st0 = jnp.concatenate([_state_rows(s_re0.astype(F32), nb, nq),
                           _state_rows(s_im0.astype(F32), nb, nq)], axis=1)
    nsl = st0.shape[1]
    half = nsl // 2
    lr = jnp.broadcast_to(_state_rows(lb_re[None], 1, nq)[0][:, None], (half, nb, nq, LANES))
    li = jnp.broadcast_to(_state_rows(lb_im[None], 1, nq)[0][:, None], (half, nb, nq, LANES))
    lr = lr.reshape(half, nb * nq, LANES)
    li = li.reshape(half, nb * nq, LANES)
    n_in = D // LANES
    kern = functools.partial(_s5_kernel, nb=nb, nq=nq, tc=tc)
    full = lambda shape: pl.BlockSpec(shape, lambda gi, i: (0,) * len(shape))
    y, stout = pl.pallas_call(
        kern,
        grid=(ng, L // tc),
        in_specs=[
            pl.BlockSpec((nb, tc, D), lambda gi, i: (gi, i, 0)),
            full((1, D)),
            full(wb.shape),
            full(wc.shape),
            full(lr.shape),
            full(li.shape),
            full((1, D)),
            pl.BlockSpec((None, nsl, SUBLANES, LANES), lambda gi, i: (gi, 0, 0, 0)),
        ],
        out_specs=[
            pl.BlockSpec((nb, tc, D), lambda gi, i: (gi, i, 0)),
            pl.BlockSpec((None, nsl, SUBLANES, LANES), lambda gi, i: (gi, 0, 0, 0)),
        ],
        out_shape=[jax.ShapeDtypeStruct((n_seq, L, D), F32),
                   jax.ShapeDtypeStruct((ng, nsl, SUBLANES, LANES), F32)],
        scratch_shapes=[pltpu.VMEM((n_in, tc * nb, LANES), F32),
                        pltpu.VMEM((nsl, tc * SUBLANES, LANES), F32),
                        pltpu.VMEM((nsl, SUBLANES, LANES), F32)],
        compiler_params=_cparams(2),
        name="s5_mixer",
    )(h, g.reshape(1, D), wb, wc, lr, li, d_skip.reshape(1, D).astype(F32), st0)
    s_re = _state_unrows(stout[:, :half], nb, nq, n_g)
    s_im = _state_unrows(stout[:, half:], nb, nq, n_g)
    return y, s_re, s_im


def _glu_kernel(y_ref, h_ref, w1_ref, w2_ref, out_ref, gy_scr):
    @pl.when(pl.program_id(1) == 0)
    def _():
        gy_scr[...] = jax.nn.gelu(y_ref[...], approximate=True).astype(BF16)

    gy = gy_scr[...]
    z1 = jnp.dot(gy, w1_ref[...], preferred_element_type=F32)
    z2 = jnp.dot(gy, w2_ref[...], preferred_element_type=F32)
    out_ref[...] = h_ref[...] + z1 * jax.nn.sigmoid(z2)


def _glu_layer(y, h, w_glu_bf, *, tm, tn=512):
    M, D = h.shape
    nn = D // tn
    return pl.pallas_call(
        _glu_kernel,
        grid=(M // tm, nn),
        in_specs=[
            pl.BlockSpec((tm, D), lambda i, n: (i, 0)),
            pl.BlockSpec((tm, tn), lambda i, n: (i, n)),
            pl.BlockSpec((D, tn), lambda i, n: (0, n)),
            pl.BlockSpec((D, tn), lambda i, n: (0, nn + n)),
        ],
        out_specs=pl.BlockSpec((tm, tn), lambda i, n: (i, n)),
        out_shape=jax.ShapeDtypeStruct((M, D), F32),
        scratch_shapes=[pltpu.VMEM((tm, D), BF16)],
        compiler_params=_cparams(2),
        name="glu",
    )(y, h, w_glu_bf, w_glu_bf)


def _ffn_kernel(h_ref, g_ref, wa_ref, wg_ref, cwa_ref, cwg_ref, cba_ref, cbg_ref, wd_ref,
                c0a_ref, c0g_ref, out_ref, coa_ref, cog_ref,
                xn_scr, za_scr, zg_scr, cara_scr, carg_scr, *, nbt, seg, tpb, nsub):
    i = pl.program_id(0)
    f = pl.program_id(1)
    pad = SUBLANES
    ctx = CONV_W - 1

    @pl.when(f == 0)
    def _():
        x = h_ref[...]
        xn_scr[...] = _rms(x, g_ref[...]).astype(BF16)
        out_ref[...] = x

    xn = xn_scr[...]
    fc = wd_ref.shape[0]
    first = (i % tpb) == 0

    def up(w_ref, c0_ref, co_ref, z_scr, car_scr, sl):
        z = jnp.dot(xn, w_ref[:, sl], preferred_element_type=F32).reshape(nbt, seg, -1)
        z_scr[:, pad:, sl] = z
        if tpb == 1:
            z_scr[:, pad - ctx:pad, sl] = c0_ref[:, :, sl]
        else:
            z_scr[:, pad - ctx:pad, sl] = jnp.where(first, c0_ref[:, :, sl],
                                                    car_scr[f, pad - ctx:pad, sl][None])
            car_scr[f, pad - ctx:pad, sl] = z[0, seg - ctx:, :]
        co_ref[:, :, sl] = z[:, seg - ctx:, :]

    def conv(cw_ref, cb_ref, z_scr, sl):
        c = cb_ref[:, sl]
        for k in range(CONV_W):
            c = c + cw_ref[k:k + 1, sl] * z_scr[:, pad - ctx + k:pad - ctx + k + seg, sl]
        return c.reshape(nbt * seg, -1)

    subs = [slice(u * (fc // nsub), (u + 1) * (fc // nsub)) for u in range(nsub)]
    for sl in subs:
        up(wa_ref, c0a_ref, coa_ref, za_scr, cara_scr, sl)
        up(wg_ref, c0g_ref, cog_ref, zg_scr, carg_scr, sl)
    acc = None
    for sl in subs:
        ca = conv(cwa_ref, cba_ref, za_scr, sl)
        cg = conv(cwg_ref, cbg_ref, zg_scr, sl)
        hg = 0.5 * cg
        act = ((hg + hg * jnp.tanh(hg)) * ca).astype(BF16)
        d = jnp.dot(act, wd_ref[sl, :], preferred_element_type=F32)
        acc = d if acc is None else acc + d
    out_ref[...] += acc


def _ffn_layer(h, conv0, g, w_up_bf, conv_w, conv_b, w_down_bf, *, n_seq, nbt, seg, fc=512,
               nsub=2):
    M, D = h.shape
    dff = w_down_bf.shape[0]
    nf = dff // fc
    L = M // n_seq
    tm = nbt * seg
    tpb = L // seg
    cwa, cwg = conv_w[:, :dff].astype(F32), conv_w[:, dff:].astype(F32)
    cba, cbg = conv_b[:dff].reshape(1, dff).astype(F32), conv_b[dff:].reshape(1, dff).astype(F32)
    c0a, c0g = conv0[:, :, :dff].astype(F32), conv0[:, :, dff:].astype(F32)
    ctx = CONV_W - 1
    kern = functools.partial(_ffn_kernel, nbt=nbt, seg=seg, tpb=tpb, nsub=nsub)
    cmap = lambda i, f: (i // tpb, 0, f)
    out, coa, cog = pl.pallas_call(
        kern,
        grid=(M // tm, nf),
        in_specs=[
            pl.BlockSpec((tm, D), lambda i, f: (i, 0)),
            pl.BlockSpec((1, D), lambda i, f: (0, 0)),
            pl.BlockSpec((D, fc), lambda i, f: (0, f)),
            pl.BlockSpec((D, fc), lambda i, f: (0, nf + f)),
            pl.BlockSpec((CONV_W, fc), lambda i, f: (0, f)),
            pl.BlockSpec((CONV_W, fc), lambda i, f: (0, f)),
            pl.BlockSpec((1, fc), lambda i, f: (0, f)),
            pl.BlockSpec((1, fc), lambda i, f: (0, f)),
            pl.BlockSpec((fc, D), lambda i, f: (f, 0)),
            pl.BlockSpec((nbt, ctx, fc), cmap),
            pl.BlockSpec((nbt, ctx, fc), cmap),
        ],
        out_specs=[
            pl.BlockSpec((tm, D), lambda i, f: (i, 0)),
            pl.BlockSpec((None, nbt, ctx, fc), lambda i, f: (i, 0, 0, f)),
            pl.BlockSpec((None, nbt, ctx, fc), lambda i, f: (i, 0, 0, f)),
        ],
        out_shape=[jax.ShapeDtypeStruct((M, D), F32),
                   jax.ShapeDtypeStruct((M // tm, nbt, ctx, dff), F32),
                   jax.ShapeDtypeStruct((M // tm, nbt, ctx, dff), F32)],
        scratch_shapes=[pltpu.VMEM((tm, D), BF16),
                        pltpu.VMEM((nbt, seg + SUBLANES, fc), F32),
                        pltpu.VMEM((nbt, seg + SUBLANES, fc), F32),
                        pltpu.VMEM((nf, SUBLANES, fc), F32),
                        pltpu.VMEM((nf, SUBLANES, fc), F32)],
        compiler_params=_cparams(2),
        name="conv_ffn",
    )(h, g.reshape(1, D), w_up_bf, w_up_bf, cwa, cwg, cba, cbg, w_down_bf, c0a, c0g)
    co = jnp.concatenate([coa, cog], axis=-1)[tpb - 1::tpb]
    return out, co.reshape(n_seq, ctx, 2 * dff)


def _ple_kernel(h_ref, p_ref, g_ref, wg_ref, wp_ref, gf_ref, out_ref, *, final, tn):
    x = h_ref[...]
    n = _rms(x, g_ref[...]).astype(BF16)
    pb = p_ref[...].astype(BF16)
    for c in range(x.shape[-1] // tn):
        sl = slice(c * tn, (c + 1) * tn)
        gate = jax.nn.sigmoid(jnp.dot(n, wg_ref[:, sl], preferred_element_type=F32))
        pr = jnp.dot(pb, wp_ref[:, sl], preferred_element_type=F32)
        out_ref[:, sl] = x[:, sl] + pr * gate
    if final:
        out_ref[...] = _rms(out_ref[...], gf_ref[...])


def _ple_layer(h, p, g, w_gate_bf, w_proj_bf, g_final, *, final, tm, tn=512):
    M, D = h.shape
    pd = p.shape[-1]
    kern = functools.partial(_ple_kernel, final=final, tn=tn)
    return pl.pallas_call(
        kern,
        grid=(M // tm,),
        in_specs=[
            pl.BlockSpec((tm, D), lambda i: (i, 0)),
            pl.BlockSpec((tm, pd), lambda i: (i, 0)),
            pl.BlockSpec((1, D), lambda i: (0, 0)),
            pl.BlockSpec((D, D), lambda i: (0, 0)),
            pl.BlockSpec((pd, D), lambda i: (0, 0)),
            pl.BlockSpec((1, D), lambda i: (0, 0)),
        ],
        out_specs=pl.BlockSpec((tm, D), lambda i: (i, 0)),
        out_shape=jax.ShapeDtypeStruct((M, D), F32),
        compiler_params=_cparams(1),
        name="ple",
    )(h, p, g.reshape(1, D), w_gate_bf, w_proj_bf, g_final.reshape(1, D))


def _rope_tables(pos):
    rot = LANES // 4
    hr = rot // 2
    inv = ROPE_THETA ** (-jnp.arange(hr, dtype=F32) * 2.0 / rot)
    ang = pos.astype(F32)[:, None] * inv[None, :]
    cos, sin = jnp.cos(ang), jnp.sin(ang)
    n = pos.shape[0]
    one = jnp.ones((n, LANES - rot), F32)
    zero_h = jnp.zeros((n, hr), F32)
    zero_r = jnp.zeros((n, LANES - rot), F32)
    c = jnp.concatenate([cos, cos, one], axis=-1)
    s1 = jnp.concatenate([-sin, zero_h, zero_r], axis=-1)
    s2 = jnp.concatenate([zero_h, sin, zero_r], axis=-1)
    return c, s1, s2


def _proj_kernel(*refs, rope, scale, f32_out, bf_out, tn):
    refs = list(refs)
    h_ref, g_ref, w_ref = refs[:3]
    refs = refs[3:]
    if rope:
        c_ref, s1_ref, s2_ref = refs[:3]
        refs = refs[3:]
    of_ref = refs.pop(0) if f32_out else None
    ob_ref = refs.pop(0) if bf_out else None
    n = _rms(h_ref[...], g_ref[...]).astype(BF16)
    hr = LANES // 8
    for c in range(w_ref.shape[-1] // tn):
        sl = slice(c * tn, (c + 1) * tn)
        r = jnp.dot(n, w_ref[:, sl], preferred_element_type=F32)
        if rope:
            cs, s1, s2 = c_ref[...], s1_ref[...], s2_ref[...]
            parts = []
            for s in range(tn // LANES):
                xb = r[:, s * LANES:(s + 1) * LANES]
                parts.append(xb * cs + pltpu.roll(xb, LANES - hr, 1) * s1 + pltpu.roll(xb, hr, 1) * s2)
            r = jnp.concatenate(parts, axis=-1)
        if f32_out:
            vd = of_ref.shape[-1]
            for hh in range(tn // vd):
                of_ref[:, c * (tn // vd) + hh, :] = r[:, hh * vd:(hh + 1) * vd]
        if bf_out:
            ob_ref[:, sl] = (r * scale).astype(BF16) if scale != 1.0 else r.astype(BF16)


def _proj_layer(h, g, w_bf, tabs, *, scale=1.0, f32_out, bf_out, tm, tn=512):
    M, D = h.shape
    N = w_bf.shape[-1]
    rope = tabs is not None
    kern = functools.partial(_proj_kernel, rope=rope, scale=scale, f32_out=f32_out,
                             bf_out=bf_out, tn=tn)
    in_specs = [pl.BlockSpec((tm, D), lambda i: (i, 0)),
                pl.BlockSpec((1, D), lambda i: (0, 0)),
                pl.BlockSpec((D, N), lambda i: (0, 0))]
    args = [h, g.reshape(1, D), w_bf]
    if rope:
        in_specs += [pl.BlockSpec((tm, LANES), lambda i: (i, 0))] * 3
        args += list(tabs)
    out_specs, out_shape = [], []
    if f32_out:
        out_specs.append(pl.BlockSpec((tm, N_HEADS, N // N_HEADS), lambda i: (i, 0, 0)))
        out_shape.append(jax.ShapeDtypeStruct((M, N_HEADS, N // N_HEADS), F32))
    if bf_out:
        out_specs.append(pl.BlockSpec((tm, N), lambda i: (i, 0)))
        out_shape.append(jax.ShapeDtypeStruct((M, N), BF16))
    return pl.pallas_call(
        kern, grid=(M // tm,), in_specs=in_specs, out_specs=out_specs, out_shape=out_shape,
        compiler_params=_cparams(1), name="norm_proj",
    )(*args)


def _softmax_tile(qs, k, v, ms, ls, accs, mask):
    new_m, new_l = [], []
    for c in range(2):
        s = lax.dot_general(qs[c], k[:, c * LANES:(c + 1) * LANES], (((1,), (1,)), ((), ())),
                            preferred_element_type=F32)
        if mask is not None:
            s = jnp.where(mask, s, NEG_BIG)
        mn = jnp.maximum(ms[c], jnp.max(s, axis=-1, keepdims=True))
        a = jnp.exp2(ms[c] - mn)
        p = jnp.exp2(s - mn)
        new_l.append(a * ls[c] + jnp.sum(p, axis=-1, keepdims=True))
        accs[c][...] = a * accs[c][...] + jnp.dot(p.astype(BF16), v, preferred_element_type=F32)
        new_m.append(mn)
    return tuple(new_m), tuple(new_l)


def _attn_finish(ls, accs, lam_ref, gs_ref, o_ref, lam_init):
    o = accs[0][...] / ls[0] - lam_ref[...] * (accs[1][...] / ls[1])
    o = _rms(o, gs_ref[...], SUBLN_EPS) * (1.0 - lam_init)
    o_ref[...] = o.astype(o_ref.dtype)


def _attn_prompt_kernel(q_ref, k_ref, v_ref, lam_ref, gs_ref, o_ref,
                        s_scr, p_scr, al_scr, mt_scr, m_scr, l_scr, acc_scr, *, tq, rb, lam_init):
    qi = pl.program_id(2)
    n_lt = tq // LANES
    m_scr[...] = jnp.full_like(m_scr, NEG_BIG)
    l_scr[...] = jnp.zeros_like(l_scr)
    acc_scr[...] = jnp.zeros_like(acc_scr)

    def scores(t, slot, masked=False):
        off = pl.multiple_of(t * tq, tq)
        for c in range(2):
            kc = k_ref[pl.ds(off, tq), c * LANES:(c + 1) * LANES]
            s = lax.dot_general(q_ref[:, c * LANES:(c + 1) * LANES], kc,
                                (((1,), (1,)), ((), ())), preferred_element_type=F32)
            if masked:
                row_c = lax.broadcasted_iota(jnp.int32, (tq, tq), 0) // CHUNK
                col_c = lax.broadcasted_iota(jnp.int32, (tq, tq), 1) // CHUNK
                s = jnp.where(col_c <= row_c, s, NEG_BIG)
            s_scr[slot, c] = s
            mt_scr[slot, c] = jnp.broadcast_to(jnp.max(s, axis=-1, keepdims=True), (tq, LANES))

    def softmax(slot, cs=(0, 1)):
        for c in cs:
            for r in range(tq // rb):
                rs = slice(r * rb, (r + 1) * rb)
                s = s_scr[slot, c, rs, :]
                m_prev = m_scr[c, rs, :]
                m_new = jnp.maximum(m_prev, mt_scr[slot, c, rs, :])
                alpha = jnp.exp2(m_prev - m_new)
                p = jnp.exp2(s - jnp.concatenate([m_new] * n_lt, axis=-1))
                psum = p[:, :LANES]
                for u in range(1, n_lt):
                    psum = psum + p[:, u * LANES:(u + 1) * LANES]
                l_scr[c, rs, :] = alpha * l_scr[c, rs, :] + psum
                m_scr[c, rs, :] = m_new
                al_scr[slot, c, rs, :] = alpha
                p_scr[slot, c, rs, :] = p.astype(BF16)

    def values(t, slot):
        off = pl.multiple_of(t * tq, tq)
        v = v_ref[pl.ds(off, tq), :]
        for c in range(2):
            pv = jnp.dot(p_scr[slot, c], v, preferred_element_type=F32)
            a = al_scr[slot, c]
            acc_scr[c] = jnp.concatenate([a, a], axis=-1) * acc_scr[c] + pv

    start = qi % 2
    n_pairs = qi // 2

    @pl.when(start == 1)
    def _():
        scores(0, 0)
        softmax(0)
        values(0, 0)

    @pl.when(n_pairs >= 1)
    def _():
        scores(start, 0)
        scores(start + 1, 1)
        softmax(0)

    def steady(u, _):
        a = start + 2 * (u + 1)

        @pl.when(a < qi)
        def _():
            scores(a, 0)
            softmax(1)
            values(a - 2, 0)

        @pl.when(a + 1 < qi)
        def _():
            scores(a + 1, 1)
            softmax(0)
            values(a - 1, 1)

        return 0

    lax.fori_loop(0, n_pairs - 1, steady, 0)

    @pl.when(n_pairs >= 1)
    def _():
        softmax(1)
        values(qi - 2, 0)
        values(qi - 1, 1)

    scores(qi, 0, masked=True)
    softmax(0)
    values(qi, 0)
    o = (acc_scr[0] / jnp.sum(l_scr[0], axis=-1, keepdims=True)
         - lam_ref[...] * (acc_scr[1] / jnp.sum(l_scr[1], axis=-1, keepdims=True)))
    o = _rms(o, gs_ref[...], SUBLN_EPS) * (1.0 - lam_init)
    o_ref[...] = o.astype(o_ref.dtype)


def _attn_prompt(q, k, v, lam_row, g_sub, *, lam_init, tq=512, rb=32):
    B, L, D = q.shape
    vd = D // N_HEADS
    kern = functools.partial(_attn_prompt_kernel, tq=tq, rb=rb, lam_init=lam_init)
    return pl.pallas_call(
        kern,
        grid=(B, N_HEADS, L // tq),
        in_specs=[
            pl.BlockSpec((None, tq, vd), lambda b, h, i: (b, i, h)),
            pl.BlockSpec((None, L, vd), lambda b, h, i: (b, 0, h)),
            pl.BlockSpec((None, L, vd), lambda b, h, i: (b, 0, h)),
            pl.BlockSpec((1, vd), lambda b, h, i: (0, 0)),
            pl.BlockSpec((1, vd), lambda b, h, i: (0, 0)),
        ],
        out_specs=pl.BlockSpec((None, tq, vd), lambda b, h, i: (b, i, h)),
        out_shape=jax.ShapeDtypeStruct((B, L, D), BF16),
        scratch_shapes=[pltpu.VMEM((2, 2, tq, tq), F32), pltpu.VMEM((2, 2, tq, tq), BF16),
                        pltpu.VMEM((2, 2, tq, LANES), F32), pltpu.VMEM((2, 2, tq, LANES), F32),
                        pltpu.VMEM((2, tq, LANES), F32),
                        pltpu.VMEM((2, tq, LANES), F32), pltpu.VMEM((2, tq, vd), F32)],
        compiler_params=_cparams(3),
        name="diff_attn_prompt",
    )(q, k, v, lam_row, g_sub.reshape(1, vd))


def _attn_cached_kernel(q_ref, kc_ref, vc_ref, kn_ref, vn_ref, lam_ref, gs_ref, o_ref, acc0, acc1,
                        *, past, lam_init):
    q = q_ref[...]
    sq = q.shape[0]
    qs = (q[:, :LANES], q[:, LANES:])
    accs = (acc0, acc1)
    acc0[...] = jnp.zeros_like(acc0)
    acc1[...] = jnp.zeros_like(acc1)
    m0 = jnp.full((sq, 1), NEG_BIG, F32)
    l0 = jnp.zeros((sq, 1), F32)
    q_cid = (past + lax.broadcasted_iota(jnp.int32, (sq, past), 0)) // CHUNK
    mask_c = lax.broadcasted_iota(jnp.int32, (sq, past), 1) // CHUNK <= q_cid
    carry = _softmax_tile(qs, kc_ref[...].astype(BF16), vc_ref[...].astype(BF16),
                          (m0, m0), (l0, l0), accs, mask_c)
    rc = (past + lax.broadcasted_iota(jnp.int32, (sq, sq), 0)) // CHUNK
    cc = (past + lax.broadcasted_iota(jnp.int32, (sq, sq), 1)) // CHUNK
    ms, ls = _softmax_tile(qs, kn_ref[...], vn_ref[...], carry[0], carry[1], accs, cc <= rc)
    _attn_finish(ls, accs, lam_ref, gs_ref, o_ref, lam_init)


def _attn_cached(q, k_cache, v_cache, k_new, v_new, lam_row, g_sub, *, lam_init):
    B, sq, D = q.shape
    past = k_cache.shape[1]
    vd = D // N_HEADS
    kern = functools.partial(_attn_cached_kernel, past=past, lam_init=lam_init)
    blk = lambda n: pl.BlockSpec((None, n, vd), lambda b, h: (b, 0, h))
    return pl.pallas_call(
        kern,
        grid=(B, N_HEADS),
        in_specs=[blk(sq), blk(past), blk(past), blk(sq), blk(sq),
                  pl.BlockSpec((1, vd), lambda b, h: (0, 0)),
                  pl.BlockSpec((1, vd), lambda b, h: (0, 0))],
        out_specs=blk(sq),
        out_shape=jax.ShapeDtypeStruct((B, sq, D), BF16),
        scratch_shapes=[pltpu.VMEM((sq, vd), F32), pltpu.VMEM((sq, vd), F32)],
        compiler_params=_cparams(2),
        name="diff_attn_cached",
    )(q, k_cache, v_cache, k_new, v_new, lam_row, g_sub.reshape(1, vd))


def _wo_kernel(o_ref, h_ref, w_ref, out_ref, *, tn):
    o = o_ref[...]
    for c in range(w_ref.shape[-1] // tn):
        sl = slice(c * tn, (c + 1) * tn)
        out_ref[:, sl] = h_ref[:, sl] + jnp.dot(o, w_ref[:, sl], preferred_element_type=F32)


def _wo_layer(o, h, w_bf, *, tm, tn=512):
    M, D = h.shape
    return pl.pallas_call(
        functools.partial(_wo_kernel, tn=tn),
        grid=(M // tm,),
        in_specs=[pl.BlockSpec((tm, D), lambda i: (i, 0)),
                  pl.BlockSpec((tm, D), lambda i: (i, 0)),
                  pl.BlockSpec((D, D), lambda i: (0, 0))],
        out_specs=pl.BlockSpec((tm, D), lambda i: (i, 0)),
        out_shape=jax.ShapeDtypeStruct((M, D), F32),
        compiler_params=_cparams(1),
        name="attn_out_proj",
    )(o, h, w_bf)


def _run_group(x, p, pos0, ssm_re0, ssm_im0, conv0, k_cache, v_cache, W, cfg):
    bt, L, D = x.shape
    M = bt * L
    depth = W['norm_mix'].shape[0]
    n_a = W['ssm_a_re'].shape[0]
    tm = cfg['tm']
    pos = pos0 + jnp.arange(L, dtype=jnp.int32)
    h = x.reshape(M, D)
    ssm_re, ssm_im, conv_new = [], [], []
    k_f32 = v_f32 = k_bf = v_bf = None
    for i in range(depth):
        if i < n_a:
            y, sr, si = _s5_layer(h.reshape(bt, L, D), W['norm_mix'][i], ssm_re0[i], ssm_im0[i],
                                  W['ssm_d'][i], *W['s5'][i],
                                  nb=cfg['s5_nb'], nq=cfg['s5_nq'], tc=cfg['s5_tc'])
            ssm_re.append(sr)
            ssm_im.append(si)
            h = _glu_layer(y.reshape(M, D), h, W['w_glu'][i], tm=tm)
        else:
            j = i - n_a
            tabs = tuple(jnp.tile(t, (bt, 1)) for t in _rope_tables(pos))
            if i == n_a:
                k_f32, k_bf = _proj_layer(h, W['norm_kv'], W['w_k'], tabs, f32_out=True,
                                          bf_out=True, tm=tm)
                v_f32, v_bf = _proj_layer(h, W['norm_kv'], W['w_v'], None, f32_out=True,
                                          bf_out=True, tm=tm)
            (q_bf,) = _proj_layer(h, W['norm_mix'][i], W['w_q'][j], tabs,
                                  scale=LANES ** -0.5 * math.log2(math.e),
                                  f32_out=False, bf_out=True, tm=tm)
            lam_init = 0.8 - 0.6 * math.exp(-0.3 * i)
            lam = (jnp.exp(jnp.sum(W['lambda_q1'][j].astype(F32) * W['lambda_k1'][j].astype(F32)))
                   - jnp.exp(jnp.sum(W['lambda_q2'][j].astype(F32) * W['lambda_k2'][j].astype(F32)))
                   + lam_init)
            lam_row = jnp.full((1, D // N_HEADS), lam, F32)
            q3 = q_bf.reshape(bt, L, D)
            if k_cache is None:
                o = _attn_prompt(q3, k_bf.reshape(bt, L, D), v_bf.reshape(bt, L, D), lam_row,
                                 W['norm_sub'][j], lam_init=lam_init)
            else:
                past = k_cache.shape[1]
                o = _attn_cached(q3, k_cache.reshape(bt, past, D), v_cache.reshape(bt, past, D),
                                 k_bf.reshape(bt, L, D), v_bf.reshape(bt, L, D), lam_row,
                                 W['norm_sub'][j], lam_init=lam_init)
            h = _wo_layer(o.reshape(M, D), h, W['w_o'][j], tm=tm)
        h, cs = _ffn_layer(h, conv0[i], W['norm_ffn'][i], W['w_up'][i], W['conv_w'][i],
                           W['conv_b'][i], W['w_down'][i], n_seq=bt, nbt=cfg['ffn_nbt'],
                           seg=cfg['ffn_seg'])
        conv_new.append(cs)
        h = _ple_layer(h, p[i].reshape(M, -1), W['norm_ple'][i], W['w_ple_gate'][i],
                       W['w_ple_proj'][i], W['norm_final'], final=(i == depth - 1), tm=tm)
    vd = D // N_HEADS
    return (h.reshape(bt, L, D), jnp.stack(ssm_re), jnp.stack(ssm_im), jnp.stack(conv_new),
            k_f32.reshape(bt, L, N_HEADS, vd), v_f32.reshape(bt, L, N_HEADS, vd))


def _group_cfg(bt, L):
    if L >= 1024:
        return dict(tm=512, s5_nb=bt, s5_nq=SUBLANES // bt, s5_tc=128, ffn_nbt=1, ffn_seg=512)
    return dict(tm=bt * L, s5_nb=SUBLANES, s5_nq=1, s5_tc=L, ffn_nbt=bt, ffn_seg=L)


def kernel(x_prompt, x_sample, state_ssm_re, state_ssm_im, state_conv, cache_k, cache_v, p_prompt, p_sample, norm_mix, ssm_a_re, ssm_a_im, ssm_log_dt, ssm_b_re, ssm_b_im, ssm_c_re, ssm_c_im, ssm_d, w_glu, norm_kv, w_k, w_v, w_q, lambda_q1, lambda_k1, lambda_q2, lambda_k2, norm_sub, w_o, norm_ffn, w_up, conv_w, conv_b, w_down, norm_ple, w_ple_gate, w_ple_proj, norm_final):
    n_a = ssm_a_re.shape[0]
    depth = norm_mix.shape[0]
    W = {
        'norm_mix': norm_mix, 'ssm_d': ssm_d, 'norm_kv': norm_kv,
        'lambda_q1': lambda_q1, 'lambda_k1': lambda_k1, 'lambda_q2': lambda_q2,
        'lambda_k2': lambda_k2, 'norm_sub': norm_sub, 'norm_ffn': norm_ffn, 'conv_w': conv_w,
        'conv_b': conv_b, 'norm_ple': norm_ple, 'norm_final': norm_final,
        'ssm_a_re': ssm_a_re,
        'w_glu': w_glu.astype(BF16), 'w_k': w_k.astype(BF16), 'w_v': w_v.astype(BF16),
        'w_q': w_q.astype(BF16), 'w_o': w_o.astype(BF16), 'w_up': w_up.astype(BF16),
        'w_down': w_down.astype(BF16), 'w_ple_gate': w_ple_gate.astype(BF16),
        'w_ple_proj': w_ple_proj.astype(BF16),
        's5': [_s5_weights(ssm_a_re[i], ssm_a_im[i], ssm_log_dt[i], ssm_b_re[i], ssm_b_im[i],
                           ssm_c_re[i], ssm_c_im[i]) for i in range(n_a)],
    }
    bp, lp, _ = x_prompt.shape
    bs, ls, _ = x_sample.shape
    n_g = ssm_a_re.shape[1]
    z_ssm = jnp.zeros((n_a, bp, n_g, SSM_STATE), F32)
    z_conv = jnp.zeros((depth, bp, CONV_W - 1, conv_w.shape[-1]), F32)
    y_p, sre_p, sim_p, conv_p, k_p, v_p = _run_group(
        x_prompt, p_prompt, 0, z_ssm, z_ssm, z_conv, None, None, W, _group_cfg(bp, lp))
    y_s, sre_s, sim_s, conv_s, k_s, v_s = _run_group(
        x_sample, p_sample, cache_k.shape[1], state_ssm_re, state_ssm_im, state_conv,
        cache_k, cache_v, W, _group_cfg(bs, ls))
    return (y_p, y_s, sre_p, sim_p, conv_p, k_p, v_p, sre_s, sim_s, conv_s, k_s, v_s)
```
